```python
import math
import jax, jax.numpy as jnp
from jax import lax
import numpy as np

D_MODEL = 1024
BATCH = 8
SEQ = 8192
DEPTH = 4

N_A = DEPTH // 2
N_B = DEPTH - N_A
MIX_W = D_MODEL
MEM_W = D_MODEL
IN_W = MIX_W + MEM_W
POOL_WINDOWS = (2, 4, 8, 16)
N_POOL = len(POOL_WINDOWS)
POOL_C = MIX_W // N_POOL
DIFF_HEADS = 4
DIFF_HD = 128
DIFF_VD = 2 * DIFF_HD
Q_BLOCK = 128
SUBLN_EPS = 1e-5
ROPE_THETA = 500000.0
ROT_DIM = DIFF_HD // 4
N_MEM = 256
MEM_HEADS = 4
MEM_HD = MEM_W // MEM_HEADS
D_FF = ((8 * D_MODEL // 3 + 255) // 256) * 256
NORM_EPS = 1e-6

kernel_name = "yoco_pool_diffattn_memory_block"


def rmsnorm(x, g, eps=NORM_EPS):
    xf = x.astype(jnp.float32)
    y = xf * lax.rsqrt(jnp.mean(xf * xf, axis=-1, keepdims=True) + eps)
    return (y * g.astype(jnp.float32)).astype(x.dtype)


def rope_tables(seq_len):
    pos = jnp.arange(seq_len, dtype=jnp.float32)
    inv_freq = jnp.power(jnp.float32(ROPE_THETA), -jnp.arange(0, ROT_DIM, 2, dtype=jnp.float32) / ROT_DIM)
    ang = pos[:, None] * inv_freq[None, :]
    ang = jnp.concatenate([ang, ang], axis=-1)
    return jnp.cos(ang), jnp.sin(ang)


def apply_partial_rope(t, cos, sin):
    shp = (1, t.shape[1]) + (1,) * (t.ndim - 3) + (ROT_DIM,)
    c = cos.reshape(shp)
    s = sin.reshape(shp)
    tr = t[..., :ROT_DIM].astype(jnp.float32)
    half = ROT_DIM // 2
    rot = jnp.concatenate([-tr[..., half:], tr[..., :half]], axis=-1)
    return jnp.concatenate([(tr * c + rot * s).astype(t.dtype), t[..., ROT_DIM:]], axis=-1)


def multi_scale_pool(u, w_pool, scale):
    b, s, _ = u.shape
    uf = u.astype(jnp.float32).reshape(b, s, N_POOL, POOL_C)
    csum = lax.cumsum(uf, axis=1)
    t = jnp.arange(s)
    outs = []
    for g, w in enumerate(POOL_WINDOWS):
        cg = csum[:, :, g]
        lagged = jnp.pad(cg, ((0, 0), (w, 0), (0, 0)))[:, :s]
        cnt = jnp.minimum(t + 1, w).astype(jnp.float32)[None, :, None]
        outs.append((cg - lagged) / cnt - uf[:, :, g])
    pooled = jnp.stack(outs, axis=2).astype(u.dtype)
    y = jnp.einsum('bsgc,gce->bsge', pooled, w_pool).reshape(b, s, MIX_W)
    return y * scale


def shared_kv(h, g_kv, w_kv, cos, sin):
    b, s, _ = h.shape
    kv = rmsnorm(h, g_kv) @ w_kv
    k = kv[..., :DIFF_HEADS * 2 * DIFF_HD].reshape(b, s, DIFF_HEADS, 2, DIFF_HD)
    v = kv[..., DIFF_HEADS * 2 * DIFF_HD:].reshape(b, s, DIFF_HEADS, DIFF_VD)
    return apply_partial_rope(k, cos, sin), v


def diff_attention(zq, k, v, lq1, lk1, lq2, lk2, g_sub, lambda_init, cos, sin):
    b, s, _ = zq.shape
    q = apply_partial_rope(zq.reshape(b, s, DIFF_HEADS, 2, DIFF_HD), cos, sin)
    lam = (jnp.exp(jnp.sum(lq1.astype(jnp.float32) * lk1.astype(jnp.float32)))
           - jnp.exp(jnp.sum(lq2.astype(jnp.float32) * lk2.astype(jnp.float32)))
           + lambda_init)
    scale = DIFF_HD ** -0.5
    nblk = s // Q_BLOCK
    qb = q.reshape(b, nblk, Q_BLOCK, DIFF_HEADS, 2, DIFF_HD).transpose(1, 0, 2, 3, 4, 5)
    kpos = jnp.arange(s)

    def one_block(args):
        qi, bi = args
        qpos = bi * Q_BLOCK + jnp.arange(Q_BLOCK)
        sc = jnp.einsum('bqhcd,bkhcd->bhcqk', qi, k).astype(jnp.float32) * scale
        mask = kpos[None, :] <= qpos[:, None]
        sc = jnp.where(mask[None, None, None], sc, -jnp.inf)
        p = jax.nn.softmax(sc, axis=-1)
        a = p[:, :, 0] - lam * p[:, :, 1]
        return jnp.einsum('bhqk,bkhe->bqhe', a.astype(v.dtype), v)

    o = lax.map(one_block, (qb, jnp.arange(nblk)))
    o = o.transpose(1, 0, 2, 3, 4).reshape(b, s, DIFF_HEADS, DIFF_VD)
    o = rmsnorm(o, g_sub, SUBLN_EPS) * (1.0 - lambda_init)
    return o.reshape(b, s, MIX_W)


def memory_attention(zq, mem, g_mem, w_mem_kv):
    b, s, _ = zq.shape
    q = zq.reshape(b, s, MEM_HEADS, MEM_HD)
    kv = rmsnorm(mem, g_mem) @ w_mem_kv
    m = mem.shape[1]
    km = kv[..., :MEM_W].reshape(b, m, MEM_HEADS, MEM_HD)
    vm = kv[..., MEM_W:].reshape(b, m, MEM_HEADS, MEM_HD)
    sc = jnp.einsum('bshd,bmhd->bhsm', q, km).astype(jnp.float32) * (MEM_HD ** -0.5)
    p = jax.nn.softmax(sc, axis=-1)
    o = jnp.einsum('bhsm,bmhd->bshd', p.astype(vm.dtype), vm)
    return o.reshape(b, s, MEM_W)


def swiglu(x, w_gate, w_up, w_down):
    return (jax.nn.silu(x @ w_gate) * (x @ w_up)) @ w_down


def setup_inputs(seed: int = 0) -> dict:
    key = jax.random.key(seed)
    ks = jax.random.split(key, 24)
    f32 = jnp.float32

    def nrm(k, shape, fan_in):
        return jax.random.normal(k, shape, f32) * (fan_in ** -0.5)

    def gain(k, shape):
        return 1.0 + 0.02 * jax.random.normal(k, shape, f32)

    return {
        "x": jax.random.normal(ks[0], (BATCH, SEQ, D_MODEL), f32),
        "mem": jax.random.normal(ks[1], (BATCH, N_MEM, D_MODEL), f32),
        "w_in": nrm(ks[2], (DEPTH, D_MODEL, IN_W), D_MODEL),
        "w_out": nrm(ks[3], (DEPTH, IN_W, D_MODEL), IN_W),
        "g_mix": gain(ks[4], (DEPTH, D_MODEL)),
        "g_ffn": gain(ks[5], (DEPTH, D_MODEL)),
        "w_gate": nrm(ks[6], (DEPTH, D_MODEL, D_FF), D_MODEL),
        "w_up": nrm(ks[7], (DEPTH, D_MODEL, D_FF), D_MODEL),
        "w_down": nrm(ks[8], (DEPTH, D_FF, D_MODEL), D_FF),
        "g_mem": gain(ks[9], (DEPTH, D_MODEL)),
        "w_mem_kv": nrm(ks[10], (DEPTH, D_MODEL, 2 * MEM_W), D_MODEL),
        "pool_w": nrm(ks[11], (N_A, N_POOL, POOL_C, POOL_C), POOL_C),
        "pool_scale": 1.0 + 0.1 * jax.random.normal(ks[12], (N_A, MIX_W), f32),
        "g_kv": gain(ks[13], (D_MODEL,)),
        "w_kv": nrm(ks[14], (D_MODEL, DIFF_HEADS * 2 * DIFF_HD + DIFF_HEADS * DIFF_VD), D_MODEL),
        "lam_q1": 0.1 * jax.random.normal(ks[15], (N_B, DIFF_HD), f32),
        "lam_k1": 0.1 * jax.random.normal(ks[16], (N_B, DIFF_HD), f32),
        "lam_q2": 0.1 * jax.random.normal(ks[17], (N_B, DIFF_HD), f32),
        "lam_k2": 0.1 * jax.random.normal(ks[18], (N_B, DIFF_HD), f32),
        "g_subln": gain(ks[19], (N_B, DIFF_VD)),
        "g_final": gain(ks[20], (D_MODEL,)),
    }


def reference(x, mem, w_in, w_out, g_mix, g_ffn, w_gate, w_up, w_down, g_mem, w_mem_kv,
              pool_w, pool_scale, g_kv, w_kv, lam_q1, lam_k1, lam_q2, lam_k2, g_subln, g_final):
    cos, sin = rope_tables(x.shape[1])
    h = x
    k_sh = None
    v_sh = None
    for i in range(DEPTH):
        z = rmsnorm(h, g_mix[i]) @ w_in[i]
        z_mix = z[..., :MIX_W]
        z_memq = z[..., MIX_W:]
        if i < N_A:
            mix = multi_scale_pool(z_mix, pool_w[i], pool_scale[i])
        else:
            j = i - N_A
            lambda_init = 0.8 - 0.6 * math.exp(-0.3 * i)
            mix = diff_attention(z_mix, k_sh, v_sh, lam_q1[j], lam_k1[j], lam_q2[j], lam_k2[j],
                                 g_subln[j], lambda_init, cos, sin)
        mem_out = memory_attention(z_memq, mem, g_mem[i], w_mem_kv[i])
        h = h + jnp.concatenate([mix, mem_out], axis=-1) @ w_out[i]
        h = h + swiglu(rmsnorm(h, g_ffn[i]), w_gate[i], w_up[i], w_down[i])
        if i == N_A - 1:
            k_sh, v_sh = shared_kv(h, g_kv, w_kv, cos, sin)
    return rmsnorm(h, g_final)
```

```python
import functools
import math

import jax
import jax.numpy as jnp
from jax import lax
from jax.experimental import pallas as pl
from jax.experimental.pallas import tpu as pltpu

POOL_WINDOWS = (2, 4, 8, 16)
MAX_WINDOW = max(POOL_WINDOWS)
DIFF_HEADS = 4
DIFF_HD = 128
DIFF_VD = 2 * DIFF_HD
MEM_HEADS = 4
ROT_DIM = DIFF_HD // 4
ROPE_THETA = 500000.0
NORM_EPS = 1e-6
SUBLN_EPS = 1e-5
LANES = 128

VMEM_LIMIT_BYTES = 56 * 1024 * 1024

TOKEN_TILE = 512
ATTN_TILE = 512
FF_CHUNK = 256


def _params(n_axes):
    return pltpu.CompilerParams(
        dimension_semantics=("arbitrary",) * n_axes,
        vmem_limit_bytes=VMEM_LIMIT_BYTES,
    )


def _resident(shape):
    zeros = (0,) * len(shape)
    return pl.BlockSpec(shape, lambda *_: zeros, pipeline_mode=pl.Buffered(1))


def _rmsnorm(x, g, eps):
    ms = jnp.mean(x * x, axis=-1, keepdims=True)
    return x * lax.rsqrt(ms + eps) * g


def _dot(a, b):
    return jnp.dot(a, b, preferred_element_type=jnp.float32)


def _dot_nt(a, b):
    return lax.dot_general(a, b, (((1,), (1,)), ((), ())),
                           preferred_element_type=jnp.float32)


def _rope(t, cos_t, sin_lo, sin_hi):
    outs = []
    for j in range(t.shape[1] // LANES):
        tj = t[:, j * LANES:(j + 1) * LANES]
        up = pltpu.roll(tj, LANES - ROT_DIM // 2, axis=1)
        dn = pltpu.roll(tj, ROT_DIM // 2, axis=1)
        outs.append(tj * cos_t + up * sin_lo + dn * sin_hi)
    return jnp.concatenate(outs, axis=1)


def _rope_tables(seq_len):
    pos = jnp.arange(seq_len, dtype=jnp.float32)
    inv_freq = jnp.power(jnp.float32(ROPE_THETA),
                         -jnp.arange(0, ROT_DIM, 2, dtype=jnp.float32) / ROT_DIM)
    ang = pos[:, None] * inv_freq[None, :]
    cos, sin = jnp.cos(ang), jnp.sin(ang)
    half = ROT_DIM // 2
    ones = jnp.ones((seq_len, LANES - ROT_DIM), jnp.float32)
    cos_t = jnp.concatenate([cos, cos, ones], axis=1)
    sin_lo = jnp.concatenate([-sin, jnp.zeros((seq_len, LANES - half), jnp.float32)], axis=1)
    sin_hi = jnp.concatenate([jnp.zeros((seq_len, half), jnp.float32), sin,
                              jnp.zeros((seq_len, LANES - ROT_DIM), jnp.float32)], axis=1)
    return cos_t, sin_lo, sin_hi


def _mem_kv_kernel(mem_ref, g_ref, w_ref, o_ref):
    xn = _rmsnorm(mem_ref[0], g_ref[0], NORM_EPS).astype(jnp.bfloat16)
    o_ref[0, 0] = _dot(xn, w_ref[0]).astype(jnp.bfloat16)


def _mem_kv(mem, g_mem, w_mem_kv):
    b, m, d = mem.shape
    depth, _, n = w_mem_kv.shape
    return pl.pallas_call(
        _mem_kv_kernel,
        grid=(depth, b),
        in_specs=[
            pl.BlockSpec((1, m, d), lambda l, i: (i, 0, 0)),
            pl.BlockSpec((1, 1, d), lambda l, i: (l, 0, 0)),
            pl.BlockSpec((1, d, n), lambda l, i: (l, 0, 0)),
        ],
        out_specs=pl.BlockSpec((1, 1, m, n), lambda l, i: (l, i, 0, 0)),
        out_shape=jax.ShapeDtypeStruct((depth, b, m, n), jnp.bfloat16),
        compiler_params=_params(2),
        name="mem_kv",
    )(mem, g_mem.reshape(depth, 1, d), w_mem_kv)


def _memory_attention(zq, kv_ref):
    mem_w = zq.shape[1]
    hd = mem_w // MEM_HEADS
    scale = hd ** -0.5
    outs = []
    for hh in range(MEM_HEADS):
        q = zq[:, hh * hd:(hh + 1) * hd].astype(jnp.bfloat16)
        km = kv_ref[0, 0, :, hh * hd:(hh + 1) * hd]
        vm = kv_ref[0, 0, :, mem_w + hh * hd:mem_w + (hh + 1) * hd]
        s = _dot_nt(q, km) * scale
        e = jnp.exp(s - jnp.max(s, axis=-1, keepdims=True))
        p = e / jnp.sum(e, axis=-1, keepdims=True)
        outs.append(_dot(p.astype(jnp.bfloat16), vm))
    return jnp.concatenate(outs, axis=1)


def _in_pool_kernel(h_ref, g_ref, w_in_ref, pw_ref, ps_ref, kv_ref,
                    mix_ref, mo_ref, zbuf, *, tm):
    si = pl.program_id(1)
    mix_w = mix_ref.shape[2]
    pool_c = mix_w // len(POOL_WINDOWS)

    xn = _rmsnorm(h_ref[0], g_ref[...], NORM_EPS).astype(jnp.bfloat16)
    z = _dot(xn, w_in_ref[...])
    zmix = z[:, :mix_w]

    @pl.when(si == 0)
    def _():
        zbuf[0:MAX_WINDOW, :] = jnp.zeros((MAX_WINDOW, mix_w), jnp.float32)

    zbuf[MAX_WINDOW:MAX_WINDOW + tm, :] = zmix
    t = si * tm + lax.broadcasted_iota(jnp.int32, (tm, 1), 0)
    outs = []
    for g, w in enumerate(POOL_WINDOWS):
        cols = slice(g * pool_c, (g + 1) * pool_c)
        acc = zmix[:, cols]
        for j in range(1, w):
            acc = acc + zbuf[MAX_WINDOW - j:MAX_WINDOW - j + tm, cols]
        cnt = jnp.minimum(t + 1, w).astype(jnp.float32)
        pooled = (acc / cnt - zmix[:, cols]).astype(jnp.bfloat16)
        outs.append(_dot(pooled, pw_ref[g]))
    zbuf[0:MAX_WINDOW, :] = zbuf[tm:tm + MAX_WINDOW, :]
    mix = jnp.concatenate(outs, axis=1) * ps_ref[...]
    mix_ref[0] = mix.astype(jnp.bfloat16)
    mo_ref[0] = _memory_attention(z[:, mix_w:], kv_ref).astype(jnp.bfloat16)


def _in_pool(h, g, w_in, pool_w, pool_scale, mem_kv, layer, tm):
    b, s, d = h.shape
    in_w = w_in.shape[1]
    mix_w = pool_scale.shape[0]
    mem_w = in_w - mix_w
    m, kvn = mem_kv.shape[2], mem_kv.shape[3]
    tok = lambda i, j: (i, j, 0)
    return pl.pallas_call(
        functools.partial(_in_pool_kernel, tm=tm),
        grid=(b, s // tm),
        in_specs=[
            pl.BlockSpec((1, tm, d), tok),
            _resident((1, d)),
            _resident((d, in_w)),
            _resident(pool_w.shape),
            _resident((1, mix_w)),
            pl.BlockSpec((1, 1, m, kvn), lambda i, j: (layer, i, 0, 0)),
        ],
        out_specs=[pl.BlockSpec((1, tm, mix_w), tok), pl.BlockSpec((1, tm, mem_w), tok)],
        out_shape=[jax.ShapeDtypeStruct((b, s, mix_w), jnp.bfloat16),
                   jax.ShapeDtypeStruct((b, s, mem_w), jnp.bfloat16)],
        scratch_shapes=[pltpu.VMEM((tm + MAX_WINDOW, mix_w), jnp.float32)],
        compiler_params=_params(2),
        name="in_pool",
    )(h, g.reshape(1, d), w_in, pool_w, pool_scale.reshape(1, mix_w), mem_kv)


def _in_diff_kernel(h_ref, g_ref, w_in_ref, cos_ref, slo_ref, shi_ref, kv_ref,
                    q_ref, mo_ref):
    mix_w = q_ref.shape[2]
    xn = _rmsnorm(h_ref[0], g_ref[...], NORM_EPS).astype(jnp.bfloat16)
    z = _dot(xn, w_in_ref[...])
    q = _rope(z[:, :mix_w], cos_ref[...], slo_ref[...], shi_ref[...])
    q_ref[0] = (q * (DIFF_HD ** -0.5)).astype(jnp.bfloat16)
    mo_ref[0] = _memory_attention(z[:, mix_w:], kv_ref).astype(jnp.bfloat16)


def _in_diff(h, g, w_in, mix_w, rope, mem_kv, layer, tm):
    b, s, d = h.shape
    in_w = w_in.shape[1]
    mem_w = in_w - mix_w
    m, kvn = mem_kv.shape[2], mem_kv.shape[3]
    tok = lambda i, j: (i, j, 0)
    tab = pl.BlockSpec((tm, LANES), lambda i, j: (j, 0))
    return pl.pallas_call(
        _in_diff_kernel,
        grid=(b, s // tm),
        in_specs=[
            pl.BlockSpec((1, tm, d), tok),
            _resident((1, d)),
            _resident((d, in_w)),
            tab, tab, tab,
            pl.BlockSpec((1, 1, m, kvn), lambda i, j: (layer, i, 0, 0)),
        ],
        out_specs=[pl.BlockSpec((1, tm, mix_w), tok), pl.BlockSpec((1, tm, mem_w), tok)],
        out_shape=[jax.ShapeDtypeStruct((b, s, mix_w), jnp.bfloat16),
                   jax.ShapeDtypeStruct((b, s, mem_w), jnp.bfloat16)],
        compiler_params=_params(2),
        name="in_diff",
    )(h, g.reshape(1, d), w_in, *rope, mem_kv)


def _shared_kv_kernel(h_ref, g_ref, w_ref, cos_ref, slo_ref, shi_ref, k_ref, v_ref):
    kw = k_ref.shape[2]
    xn = _rmsnorm(h_ref[0], g_ref[...], NORM_EPS).astype(jnp.bfloat16)
    kv = _dot(xn, w_ref[...])
    k = _rope(kv[:, :kw], cos_ref[...], slo_ref[...], shi_ref[...])
    k_ref[0] = k.astype(jnp.bfloat16)
    v_ref[0] = kv[:, kw:].astype(jnp.bfloat16)


def _shared_kv(h, g, w_kv, rope, tm):
    b, s, d = h.shape
    n = w_kv.shape[1]
    kw = DIFF_HEADS * 2 * DIFF_HD
    vw = n - kw
    tok = lambda i, j: (i, j, 0)
    tab = pl.BlockSpec((tm, LANES), lambda i, j: (j, 0))
    return pl.pallas_call(
        _shared_kv_kernel,
        grid=(b, s // tm),
        in_specs=[pl.BlockSpec((1, tm, d), tok), _resident((1, d)), _resident((d, n)),
                  tab, tab, tab],
        out_specs=[pl.BlockSpec((1, tm, kw), tok), pl.BlockSpec((1, tm, vw), tok)],
        out_shape=[jax.ShapeDtypeStruct((b, s, kw), jnp.bfloat16),
                   jax.ShapeDtypeStruct((b, s, vw), jnp.bfloat16)],
        compiler_params=_params(2),
        name="shared_kv",
    )(h, g.reshape(1, d), w_kv, *rope)


def _diff_attn_kernel(q_ref, k_ref, v_ref, lq1_ref, lk1_ref, lq2_ref, lk2_ref, gs_ref,
                      o_ref, m_sc, l_sc, acc_sc, *, tq, lambda_init):
    qi = pl.program_id(2)
    m_sc[...] = jnp.full(m_sc.shape, -jnp.inf, jnp.float32)
    l_sc[...] = jnp.zeros(l_sc.shape, jnp.float32)
    acc_sc[...] = jnp.zeros(acc_sc.shape, jnp.float32)

    def block(j, masked):
        rows = pl.ds(pl.multiple_of(j * tq, tq), tq)
        v = v_ref[0, rows, :]
        for c in range(2):
            cols = slice(c * DIFF_HD, (c + 1) * DIFF_HD)
            s = _dot_nt(q_ref[0, :, cols], k_ref[0, rows, cols])
            if masked:
                r = lax.broadcasted_iota(jnp.int32, s.shape, 0)
                col = lax.broadcasted_iota(jnp.int32, s.shape, 1)
                s = jnp.where(col <= r, s, -jnp.inf)
            m_old = m_sc[c]
            m_new = jnp.maximum(m_old, jnp.max(s, axis=-1, keepdims=True))
            alpha = jnp.exp(m_old - m_new)
            p = jnp.exp(s - m_new)
            l_sc[c] = alpha * l_sc[c] + jnp.sum(p, axis=-1, keepdims=True)
            acc_sc[c] = alpha * acc_sc[c] + _dot(p.astype(jnp.bfloat16), v)
            m_sc[c] = m_new

    def full_block(j, carry):
        block(j, False)
        return carry

    lax.fori_loop(0, qi, full_block, 0)
    block(qi, True)

    lam = (jnp.exp(jnp.sum(lq1_ref[...] * lk1_ref[...], keepdims=True))
           - jnp.exp(jnp.sum(lq2_ref[...] * lk2_ref[...], keepdims=True))
           + lambda_init)
    o = acc_sc[0] / l_sc[0] - lam * (acc_sc[1] / l_sc[1])
    o = _rmsnorm(o, gs_ref[...], SUBLN_EPS) * (1.0 - lambda_init)
    o_ref[0] = o.astype(jnp.bfloat16)


def _diff_attention(q, k, v, lq1, lk1, lq2, lk2, g_sub, lambda_init, tq):
    b, s, w = q.shape
    hw = 2 * DIFF_HD
    vec = lambda a: a.reshape(1, -1)
    return pl.pallas_call(
        functools.partial(_diff_attn_kernel, tq=tq, lambda_init=lambda_init),
        grid=(b, DIFF_HEADS, s // tq),
        in_specs=[
            pl.BlockSpec((1, tq, hw), lambda i, h, j: (i, j, h)),
            pl.BlockSpec((1, s, hw), lambda i, h, j: (i, 0, h)),
            pl.BlockSpec((1, s, DIFF_VD), lambda i, h, j: (i, 0, h)),
            _resident((1, DIFF_HD)), _resident((1, DIFF_HD)),
            _resident((1, DIFF_HD)), _resident((1, DIFF_HD)),
            _resident((1, DIFF_VD)),
        ],
        out_specs=pl.BlockSpec((1, tq, DIFF_VD), lambda i, h, j: (i, j, h)),
        out_shape=jax.ShapeDtypeStruct((b, s, w), jnp.bfloat16),
        scratch_shapes=[pltpu.VMEM((2, tq, 1), jnp.float32),
                        pltpu.VMEM((2, tq, 1), jnp.float32),
                        pltpu.VMEM((2, tq, DIFF_VD), jnp.float32)],
        compiler_params=_params(3),
        name="diff_attn",
    )(q, k, v, vec(lq1), vec(lk1), vec(lq2), vec(lk2), vec(g_sub))


def _out_ffn_kernel(h_ref, mix_ref, mo_ref, w_out_ref, g_ref, wg_ref, wu_ref, wd_ref,
                    gf_ref, o_ref, *, n_chunks, final_norm):
    mix_w = mix_ref.shape[2]
    h1 = (h_ref[0] + _dot(mix_ref[0], w_out_ref[0:mix_w, :])
          + _dot(mo_ref[0], w_out_ref[mix_w:, :]))
    xn = _rmsnorm(h1, g_ref[...], NORM_EPS).astype(jnp.bfloat16)

    def chunk(c, acc):
        gate = _dot(xn, wg_ref[c])
        up = _dot(xn, wu_ref[c])
        act = (gate * jax.nn.sigmoid(gate) * up).astype(jnp.bfloat16)
        return acc + _dot(act, wd_ref[c])

    h2 = lax.fori_loop(0, n_chunks, chunk, h1)
    if final_norm:
        h2 = _rmsnorm(h2, gf_ref[...], NORM_EPS)
    o_ref[0] = h2


def _out_ffn(h, mix, mo, w_out, g_ffn, w_gate, w_up, w_down, g_final, final_norm, tm):
    b, s, d = h.shape
    n_chunks = w_gate.shape[0]
    tok = lambda i, j: (i, j, 0)
    return pl.pallas_call(
        functools.partial(_out_ffn_kernel, n_chunks=n_chunks, final_norm=final_norm),
        grid=(b, s // tm),
        in_specs=[
            pl.BlockSpec((1, tm, d), tok),
            pl.BlockSpec((1, tm, mix.shape[2]), tok),
            pl.BlockSpec((1, tm, mo.shape[2]), tok),
            _resident(w_out.shape),
            _resident((1, d)),
            _resident(w_gate.shape), _resident(w_up.shape), _resident(w_down.shape),
            _resident((1, d)),
        ],
        out_specs=pl.BlockSpec((1, tm, d), tok),
        out_shape=jax.ShapeDtypeStruct((b, s, d), jnp.float32),
        compiler_params=_params(2),
        name="out_ffn",
    )(h, mix, mo, w_out, g_ffn.reshape(1, d), w_gate, w_up, w_down, g_final.reshape(1, d))


def kernel(x, mem, w_in, w_out, g_mix, g_ffn, w_gate, w_up, w_down, g_mem, w_mem_kv,
           pool_w, pool_scale, g_kv, w_kv, lam_q1, lam_k1, lam_q2, lam_k2, g_subln, g_final):
    b, s, d = x.shape
    depth = w_in.shape[0]
    n_a = pool_w.shape[0]
    mix_w = pool_scale.shape[1]
    d_ff = w_gate.shape[2]
    tm = min(TOKEN_TILE, s)
    tq = min(ATTN_TILE, s)
    assert s % tm == 0 and s % tq == 0 and d_ff % FF_CHUNK == 0
    n_chunks = d_ff // FF_CHUNK

    bf = lambda a: a.astype(jnp.bfloat16)
    wg = bf(w_gate).reshape(depth, d, n_chunks, FF_CHUNK).transpose(0, 2, 1, 3)
    wu = bf(w_up).reshape(depth, d, n_chunks, FF_CHUNK).transpose(0, 2, 1, 3)
    wd = bf(w_down).reshape(depth, n_chunks, FF_CHUNK, d)
    w_in_b, w_out_b, pool_w_b, w_kv_b = bf(w_in), bf(w_out), bf(pool_w), bf(w_kv)

    rope = _rope_tables(s)
    mem_kv = _mem_kv(mem, g_mem, bf(w_mem_kv))

    h = x
    k_sh = v_sh = None
    for i in range(depth):
        if i < n_a:
            mix, mo = _in_pool(h, g_mix[i], w_in_b[i], pool_w_b[i], pool_scale[i], mem_kv, i, tm)
        else:
            j = i - n_a
            lambda_init = 0.8 - 0.6 * math.exp(-0.3 * i)
            q, mo = _in_diff(h, g_mix[i], w_in_b[i], mix_w, rope, mem_kv, i, tm)
            mix = _diff_attention(q, k_sh, v_sh, lam_q1[j], lam_k1[j], lam_q2[j], lam_k2[j],
                                  g_subln[j], lambda_init, tq)
        h = _out_ffn(h, mix, mo, w_out_b[i], g_ffn[i], wg[i], wu[i], wd[i], g_final,
                     i == depth - 1, tm)
        if i == n_a - 1:
            k_sh, v_sh = _shared_kv(h, g_kv, w_kv_b, rope, tm)
    return h
```

```python
import functools
import math

import jax
import jax.numpy as jnp
import numpy as np
from jax import lax
from jax.experimental import pallas as pl
from jax.experimental.pallas import tpu as pltpu

POOL_WINDOWS = (2, 4, 8, 16)
MAX_WINDOW = max(POOL_WINDOWS)
DIFF_HEADS = 4
DIFF_HD = 128
DIFF_VD = 2 * DIFF_HD
MEM_HEADS = 4
ROT_DIM = DIFF_HD // 4
ROPE_THETA = 500000.0
NORM_EPS = 1e-6
SUBLN_EPS = 1e-5
LOG2E = math.log2(math.e)
LANES = 128

VMEM_LIMIT_BYTES = 56 * 1024 * 1024

TOKEN_TILE = 512
ATTN_TILE = 512
ATTN_STRIP = 32
FF_CHUNK = 256


def _params(n_axes):
    return pltpu.CompilerParams(
        dimension_semantics=("arbitrary",) * n_axes,
        vmem_limit_bytes=VMEM_LIMIT_BYTES,
    )


def _resident(shape):
    zeros = (0,) * len(shape)
    return pl.BlockSpec(shape, lambda *_: zeros, pipeline_mode=pl.Buffered(1))


def _rmsnorm(x, g, eps):
    ms = jnp.mean(x * x, axis=-1, keepdims=True)
    return x * lax.rsqrt(ms + eps) * g


def _dot(a, b):
    return jnp.dot(a, b, preferred_element_type=jnp.float32)


def _dot_nt(a, b):
    return lax.dot_general(a, b, (((1,), (1,)), ((), ())),
                           preferred_element_type=jnp.float32)


def _rope(t, cos_t, sin_lo, sin_hi):
    outs = []
    for j in range(t.shape[1] // LANES):
        tj = t[:, j * LANES:(j + 1) * LANES]
        up = pltpu.roll(tj, LANES - ROT_DIM // 2, axis=1)
        dn = pltpu.roll(tj, ROT_DIM // 2, axis=1)
        outs.append(tj * cos_t + up * sin_lo + dn * sin_hi)
    return jnp.concatenate(outs, axis=1)


def _rope_tables(seq_len):
    pos = jnp.arange(seq_len, dtype=jnp.float32)
    inv_freq = jnp.power(jnp.float32(ROPE_THETA),
                         -jnp.arange(0, ROT_DIM, 2, dtype=jnp.float32) / ROT_DIM)
    ang = pos[:, None] * inv_freq[None, :]
    cos, sin = jnp.cos(ang), jnp.sin(ang)
    half = ROT_DIM // 2
    ones = jnp.ones((seq_len, LANES - ROT_DIM), jnp.float32)
    cos_t = jnp.concatenate([cos, cos, ones], axis=1)
    sin_lo = jnp.concatenate([-sin, jnp.zeros((seq_len, LANES - half), jnp.float32)], axis=1)
    sin_hi = jnp.concatenate([jnp.zeros((seq_len, half), jnp.float32), sin,
                              jnp.zeros((seq_len, LANES - ROT_DIM), jnp.float32)], axis=1)
    return cos_t, sin_lo, sin_hi


def _mem_kv_kernel(mem_ref, g_ref, w_ref, o_ref):
    xn = _rmsnorm(mem_ref[0], g_ref[0], NORM_EPS).astype(jnp.bfloat16)
    o_ref[0, 0] = _dot(xn, w_ref[0]).astype(jnp.bfloat16)


def _mem_kv(mem, g_mem, w_mem_kv):
    b, m, d = mem.shape
    depth, _, n = w_mem_kv.shape
    return pl.pallas_call(
        _mem_kv_kernel,
        grid=(depth, b),
        in_specs=[
            pl.BlockSpec((1, m, d), lambda l, i: (i, 0, 0)),
            pl.BlockSpec((1, 1, d), lambda l, i: (l, 0, 0)),
            pl.BlockSpec((1, d, n), lambda l, i: (l, 0, 0)),
        ],
        out_specs=pl.BlockSpec((1, 1, m, n), lambda l, i: (l, i, 0, 0)),
        out_shape=jax.ShapeDtypeStruct((depth, b, m, n), jnp.bfloat16),
        compiler_params=_params(2),
        name="mem_kv",
    )(mem, g_mem.reshape(depth, 1, d), w_mem_kv)


def _memory_attention(zq, kv_ref):
    mem_w = zq.shape[1]
    hd = mem_w // MEM_HEADS
    scale = hd ** -0.5
    outs = []
    for hh in range(MEM_HEADS):
        q = zq[:, hh * hd:(hh + 1) * hd].astype(jnp.bfloat16)
        km = kv_ref[0, 0, :, hh * hd:(hh + 1) * hd]
        vm = kv_ref[0, 0, :, mem_w + hh * hd:mem_w + (hh + 1) * hd]
        s = _dot_nt(q, km) * scale
        e = jnp.exp(s - jnp.max(s, axis=-1, keepdims=True))
        p = e / jnp.sum(e, axis=-1, keepdims=True)
        outs.append(_dot(p.astype(jnp.bfloat16), vm))
    return jnp.concatenate(outs, axis=1)


def _in_pool_kernel(h_ref, g_ref, w_in_ref, pw_ref, ps_ref, kv_ref,
                    mix_ref, mo_ref, zbuf, *, tm):
    si = pl.program_id(1)
    mix_w = mix_ref.shape[2]
    pool_c = mix_w // len(POOL_WINDOWS)

    xn = _rmsnorm(h_ref[0], g_ref[...], NORM_EPS).astype(jnp.bfloat16)
    z = _dot(xn, w_in_ref[...])
    zmix = z[:, :mix_w]

    @pl.when(si == 0)
    def _():
        zbuf[0:MAX_WINDOW, :] = jnp.zeros((MAX_WINDOW, mix_w), jnp.float32)

    zbuf[MAX_WINDOW:MAX_WINDOW + tm, :] = zmix
    t = si * tm + lax.broadcasted_iota(jnp.int32, (tm, 1), 0)
    outs = []
    for g, w in enumerate(POOL_WINDOWS):
        cols = slice(g * pool_c, (g + 1) * pool_c)
        acc = zmix[:, cols]
        for j in range(1, w):
            acc = acc + zbuf[MAX_WINDOW - j:MAX_WINDOW - j + tm, cols]
        cnt = jnp.minimum(t + 1, w).astype(jnp.float32)
        pooled = (acc / cnt - zmix[:, cols]).astype(jnp.bfloat16)
        outs.append(_dot(pooled, pw_ref[g]))
    zbuf[0:MAX_WINDOW, :] = zbuf[tm:tm + MAX_WINDOW, :]
    mix = jnp.concatenate(outs, axis=1) * ps_ref[...]
    mix_ref[0] = mix.astype(jnp.bfloat16)
    mo_ref[0] = _memory_attention(z[:, mix_w:], kv_ref).astype(jnp.bfloat16)


def _in_pool(h, g, w_in, pool_w, pool_scale, mem_kv, layer, tm):
    b, s, d = h.shape
    in_w = w_in.shape[1]
    mix_w = pool_scale.shape[0]
    mem_w = in_w - mix_w
    m, kvn = mem_kv.shape[2], mem_kv.shape[3]
    tok = lambda i, j: (i, j, 0)
    return pl.pallas_call(
        functools.partial(_in_pool_kernel, tm=tm),
        grid=(b, s // tm),
        in_specs=[
            pl.BlockSpec((1, tm, d), tok),
            _resident((1, d)),
            _resident((d, in_w)),
            _resident(pool_w.shape),
            _resident((1, mix_w)),
            pl.BlockSpec((1, 1, m, kvn), lambda i, j: (layer, i, 0, 0)),
        ],
        out_specs=[pl.BlockSpec((1, tm, mix_w), tok), pl.BlockSpec((1, tm, mem_w), tok)],
        out_shape=[jax.ShapeDtypeStruct((b, s, mix_w), jnp.bfloat16),
                   jax.ShapeDtypeStruct((b, s, mem_w), jnp.bfloat16)],
        scratch_shapes=[pltpu.VMEM((tm + MAX_WINDOW, mix_w), jnp.float32)],
        compiler_params=_params(2),
        name="in_pool",
    )(h, g.reshape(1, d), w_in, pool_w, pool_scale.reshape(1, mix_w), mem_kv)


def _in_diff_kernel(h_ref, g_ref, w_in_ref, cos_ref, slo_ref, shi_ref, kv_ref,
                    q_ref, mo_ref):
    mix_w = q_ref.shape[2]
    xn = _rmsnorm(h_ref[0], g_ref[...], NORM_EPS).astype(jnp.bfloat16)
    z = _dot(xn, w_in_ref[...])
    q = _rope(z[:, :mix_w], cos_ref[...], slo_ref[...], shi_ref[...])
    q_ref[0] = (q * (DIFF_HD ** -0.5 * LOG2E)).astype(jnp.bfloat16)
    mo_ref[0] = _memory_attention(z[:, mix_w:], kv_ref).astype(jnp.bfloat16)


def _in_diff(h, g, w_in, mix_w, rope, mem_kv, layer, tm):
    b, s, d = h.shape
    in_w = w_in.shape[1]
    mem_w = in_w - mix_w
    m, kvn = mem_kv.shape[2], mem_kv.shape[3]
    tok = lambda i, j: (i, j, 0)
    tab = pl.BlockSpec((tm, LANES), lambda i, j: (j, 0))
    return pl.pallas_call(
        _in_diff_kernel,
        grid=(b, s // tm),
        in_specs=[
            pl.BlockSpec((1, tm, d), tok),
            _resident((1, d)),
            _resident((d, in_w)),
            tab, tab, tab,
            pl.BlockSpec((1, 1, m, kvn), lambda i, j: (layer, i, 0, 0)),
        ],
        out_specs=[pl.BlockSpec((1, tm, mix_w), tok), pl.BlockSpec((1, tm, mem_w), tok)],
        out_shape=[jax.ShapeDtypeStruct((b, s, mix_w), jnp.bfloat16),
                   jax.ShapeDtypeStruct((b, s, mem_w), jnp.bfloat16)],
        compiler_params=_params(2),
        name="in_diff",
    )(h, g.reshape(1, d), w_in, *rope, mem_kv)


def _shared_kv_kernel(h_ref, g_ref, w_ref, cos_ref, slo_ref, shi_ref, k_ref, v_ref):
    kw = k_ref.shape[2]
    xn = _rmsnorm(h_ref[0], g_ref[...], NORM_EPS).astype(jnp.bfloat16)
    kv = _dot(xn, w_ref[...])
    k = _rope(kv[:, :kw], cos_ref[...], slo_ref[...], shi_ref[...])
    k_ref[0] = k.astype(jnp.bfloat16)
    v_ref[0] = kv[:, kw:].astype(jnp.bfloat16)


def _shared_kv(h, g, w_kv, rope, tm):
    b, s, d = h.shape
    n = w_kv.shape[1]
    kw = DIFF_HEADS * 2 * DIFF_HD
    vw = n - kw
    tok = lambda i, j: (i, j, 0)
    tab = pl.BlockSpec((tm, LANES), lambda i, j: (j, 0))
    return pl.pallas_call(
        _shared_kv_kernel,
        grid=(b, s // tm),
        in_specs=[pl.BlockSpec((1, tm, d), tok), _resident((1, d)), _resident((d, n)),
                  tab, tab, tab],
        out_specs=[pl.BlockSpec((1, tm, kw), tok), pl.BlockSpec((1, tm, vw), tok)],
        out_shape=[jax.ShapeDtypeStruct((b, s, kw), jnp.bfloat16),
                   jax.ShapeDtypeStruct((b, s, vw), jnp.bfloat16)],
        compiler_params=_params(2),
        name="shared_kv",
    )(h, g.reshape(1, d), w_kv, *rope)


def _attn_items(nq):
    rows = []
    for t in range(nq):
        keys = [t] + list(range(t))
        for n, j in enumerate(keys):
            rows.append((t, j, int(j == t), int(n == 0), int(n == len(keys) - 1)))
    return np.asarray(rows, np.int32).T


def _diff_attn_kernel(tbl, q_ref, k_ref, v_ref, bias_ref, lq1_ref, lk1_ref, lq2_ref, lk2_ref,
                      gs_ref, o_ref, s0, s1, p0, p1, a0, a1, m_sc, l_sc, acc_sc,
                      *, tq, n_items, lambda_init):
    m_sc[...] = jnp.zeros(m_sc.shape, jnp.float32)
    l_sc[...] = jnp.zeros(l_sc.shape, jnp.float32)
    acc_sc[...] = jnp.zeros(acc_sc.shape, jnp.float32)
    n_strips = tq // ATTN_STRIP
    n_chunks = tq // LANES
    s_buf, p_buf, a_buf = (s0, s1), (p0, p1), (a0, a1)

    def tile_rows(idx):
        return pl.ds(pl.multiple_of(idx * tq, tq), tq)

    def scores(t, slot):
        qrows, krows = tile_rows(tbl[0, t]), tile_rows(tbl[1, t])
        for c in range(2):
            cols = slice(c * DIFF_HD, (c + 1) * DIFF_HD)
            s_buf[slot][c] = _dot_nt(q_ref[0, qrows, cols], k_ref[0, krows, cols])

    def softmax(t, slot):
        diag = tbl[2, t]
        first = tbl[3, t] == 1
        par = tbl[0, t] % 2
        for c in range(2):
            for r in range(n_strips):
                rows = slice(r * ATTN_STRIP, (r + 1) * ATTN_STRIP)
                sv = s_buf[slot][c, rows, :] + bias_ref[diag, rows, :]
                m_old = jnp.where(first, -jnp.inf, m_sc[par, c, rows, :])
                m_new = jnp.maximum(m_old, jnp.max(sv, axis=-1, keepdims=True))
                alpha = jnp.exp2(m_old - m_new)
                p = jnp.exp2(sv - jnp.concatenate([m_new] * n_chunks, axis=1))
                psum = p[:, 0:LANES]
                for k in range(1, n_chunks):
                    psum = psum + p[:, k * LANES:(k + 1) * LANES]
                l_sc[par, c, rows, :] = alpha * l_sc[par, c, rows, :] + psum
                m_sc[par, c, rows, :] = m_new
                a_buf[slot][c, rows, :] = alpha
                p_buf[slot][c, rows, :] = p.astype(jnp.bfloat16)

    def values(t, slot):
        v = v_ref[0, tile_rows(tbl[1, t]), :]
        par = tbl[0, t] % 2
        for c in range(2):
            alpha = a_buf[slot][c]
            acc_sc[par, c] = (jnp.concatenate([alpha] * (DIFF_VD // LANES), axis=1)
                              * acc_sc[par, c] + _dot(p_buf[slot][c], v))

    def finalize(t):
        @pl.when(tbl[4, t] == 1)
        def _():
            lam = (jnp.exp(jnp.sum(lq1_ref[...] * lk1_ref[...], keepdims=True))
                   - jnp.exp(jnp.sum(lq2_ref[...] * lk2_ref[...], keepdims=True))
                   + lambda_init)
            par = tbl[0, t] % 2
            o1 = acc_sc[par, 0] / jnp.sum(l_sc[par, 0], axis=-1, keepdims=True)
            o2 = acc_sc[par, 1] / jnp.sum(l_sc[par, 1], axis=-1, keepdims=True)
            o = _rmsnorm(o1 - lam * o2, gs_ref[...], SUBLN_EPS) * (1.0 - lambda_init)
            o_ref[0, tile_rows(tbl[0, t]), :] = o.astype(jnp.bfloat16)

    def tick(i, parity):
        if not isinstance(i, int) or 0 <= i - 2 < n_items:
            values(i - 2, parity)
        if not isinstance(i, int) or 0 <= i - 1 < n_items:
            softmax(i - 1, 1 - parity)
        if not isinstance(i, int) or 0 <= i < n_items:
            scores(i, parity)

    def ticks(first, count):
        for d in range(count):
            tick(first + d, d % 2 if not isinstance(first, int) else (first + d) % 2)
        for d in range(count):
            if not isinstance(first, int) or 0 <= first + d - 2 < n_items:
                finalize(first + d - 2)

    ticks(0, 2)
    n_pairs = (n_items - 2) // 2
    lax.fori_loop(0, n_pairs, lambda k, carry: (ticks(2 + 2 * k, 2), carry)[1], 0)
    for i in range(2 + 2 * n_pairs, n_items + 2):
        ticks(i, 1)


def _diff_attention(q, k, v, lq1, lk1, lq2, lk2, g_sub, lambda_init, tq):
    b, s, w = q.shape
    hw = 2 * DIFF_HD
    items = _attn_items(s // tq)
    n_items = items.shape[1]
    assert n_items >= 2
    r = np.arange(tq)
    bias = np.zeros((2, tq, tq), np.float32)
    bias[1] = np.where(r[None, :] <= r[:, None], 0.0, -np.inf)
    vec = lambda a: a.reshape(1, -1)
    head = lambda i, h, tbl: (i, 0, h)
    const2 = lambda i, h, tbl: (0, 0)
    grid_spec = pltpu.PrefetchScalarGridSpec(
        num_scalar_prefetch=1,
        grid=(b, DIFF_HEADS),
        in_specs=[
            pl.BlockSpec((1, s, hw), head),
            pl.BlockSpec((1, s, hw), head),
            pl.BlockSpec((1, s, DIFF_VD), head),
            pl.BlockSpec((2, tq, tq), lambda i, h, tbl: (0, 0, 0),
                         pipeline_mode=pl.Buffered(1)),
            pl.BlockSpec((1, DIFF_HD), const2), pl.BlockSpec((1, DIFF_HD), const2),
            pl.BlockSpec((1, DIFF_HD), const2), pl.BlockSpec((1, DIFF_HD), const2),
            pl.BlockSpec((1, DIFF_VD), const2),
        ],
        out_specs=pl.BlockSpec((1, s, DIFF_VD), head),
        scratch_shapes=[
            pltpu.VMEM((2, tq, tq), jnp.float32), pltpu.VMEM((2, tq, tq), jnp.float32),
            pltpu.VMEM((2, tq, tq), jnp.bfloat16), pltpu.VMEM((2, tq, tq), jnp.bfloat16),
            pltpu.VMEM((2, tq, LANES), jnp.float32), pltpu.VMEM((2, tq, LANES), jnp.float32),
            pltpu.VMEM((2, 2, tq, LANES), jnp.float32),
            pltpu.VMEM((2, 2, tq, LANES), jnp.float32),
            pltpu.VMEM((2, 2, tq, DIFF_VD), jnp.float32),
        ],
    )
    return pl.pallas_call(
        functools.partial(_diff_attn_kernel, tq=tq, n_items=n_items, lambda_init=lambda_init),
        grid_spec=grid_spec,
        out_shape=jax.ShapeDtypeStruct((b, s, w), jnp.bfloat16),
        compiler_params=_params(2),
        name="diff_attn",
    )(jnp.asarray(items), q, k, v, jnp.asarray(bias), vec(lq1), vec(lk1), vec(lq2), vec(lk2),
      vec(g_sub))


def _out_ffn_kernel(h_ref, mix_ref, mo_ref, w_out_ref, g_ref, wg_ref, wu_ref, wd_ref,
                    gf_ref, o_ref, *, n_chunks, final_norm):
    mix_w = mix_ref.shape[2]
    h1 = (h_ref[0] + _dot(mix_ref[0], w_out_ref[0:mix_w, :])
          + _dot(mo_ref[0], w_out_ref[mix_w:, :]))
    xn = _rmsnorm(h1, g_ref[...], NORM_EPS).astype(jnp.bfloat16)

    def chunk(c, acc):
        gate = _dot(xn, wg_ref[c])
        up = _dot(xn, wu_ref[c])
        act = (gate * jax.nn.sigmoid(gate) * up).astype(jnp.bfloat16)
        return acc + _dot(act, wd_ref[c])

    h2 = lax.fori_loop(0, n_chunks, chunk, h1)
    if final_norm:
        h2 = _rmsnorm(h2, gf_ref[...], NORM_EPS)
    o_ref[0] = h2


def _out_ffn(h, mix, mo, w_out, g_ffn, w_gate, w_up, w_down, g_final, final_norm, tm):
    b, s, d = h.shape
    n_chunks = w_gate.shape[0]
    tok = lambda i, j: (i, j, 0)
    return pl.pallas_call(
        functools.partial(_out_ffn_kernel, n_chunks=n_chunks, final_norm=final_norm),
        grid=(b, s // tm),
        in_specs=[
            pl.BlockSpec((1, tm, d), tok),
            pl.BlockSpec((1, tm, mix.shape[2]), tok),
            pl.BlockSpec((1, tm, mo.shape[2]), tok),
            _resident(w_out.shape),
            _resident((1, d)),
            _resident(w_gate.shape), _resident(w_up.shape), _resident(w_down.shape),
            _resident((1, d)),
        ],
        out_specs=pl.BlockSpec((1, tm, d), tok),
        out_shape=jax.ShapeDtypeStruct((b, s, d), jnp.float32),
        compiler_params=_params(2),
        name="out_ffn",
    )(h, mix, mo, w_out, g_ffn.reshape(1, d), w_gate, w_up, w_down, g_final.reshape(1, d))


def kernel(x, mem, w_in, w_out, g_mix, g_ffn, w_gate, w_up, w_down, g_mem, w_mem_kv,
           pool_w, pool_scale, g_kv, w_kv, lam_q1, lam_k1, lam_q2, lam_k2, g_subln, g_final):
    b, s, d = x.shape
    depth = w_in.shape[0]
    n_a = pool_w.shape[0]
    mix_w = pool_scale.shape[1]
    d_ff = w_gate.shape[2]
    tm = min(TOKEN_TILE, s)
    tq = min(ATTN_TILE, s)
    assert s % tm == 0 and s % tq == 0 and d_ff % FF_CHUNK == 0
    n_chunks = d_ff // FF_CHUNK

    bf = lambda a: a.astype(jnp.bfloat16)
    wg = bf(w_gate).reshape(depth, d, n_chunks, FF_CHUNK).transpose(0, 2, 1, 3)
    wu = bf(w_up).reshape(depth, d, n_chunks, FF_CHUNK).transpose(0, 2, 1, 3)
    wd = bf(w_down).reshape(depth, n_chunks, FF_CHUNK, d)
    w_in_b, w_out_b, pool_w_b, w_kv_b = bf(w_in), bf(w_out), bf(pool_w), bf(w_kv)

    rope = _rope_tables(s)
    mem_kv = _mem_kv(mem, g_mem, bf(w_mem_kv))

    h = x
    k_sh = v_sh = None
    for i in range(depth):
        if i < n_a:
            mix, mo = _in_pool(h, g_mix[i], w_in_b[i], pool_w_b[i], pool_scale[i], mem_kv, i, tm)
        else:
            j = i - n_a
            lambda_init = 0.8 - 0.6 * math.exp(-0.3 * i)
            q, mo = _in_diff(h, g_mix[i], w_in_b[i], mix_w, rope, mem_kv, i, tm)
            mix = _diff_attention(q, k_sh, v_sh, lam_q1[j], lam_k1[j], lam_q2[j], lam_k2[j],
                                  g_subln[j], lambda_init, tq)
        h = _out_ffn(h, mix, mo, w_out_b[i], g_ffn[i], wg[i], wu[i], wd[i], g_final,
                     i == depth - 1, tm)
        if i == n_a - 1:
            k_sh, v_sh = _shared_kv(h, g_kv, w_kv_b, rope, tm)
    return h
```

```python
import functools
import math

import jax
import jax.numpy as jnp
import numpy as np
from jax import lax
from jax.experimental import pallas as pl
from jax.experimental.pallas import tpu as pltpu

POOL_WINDOWS = (2, 4, 8, 16)
MAX_WINDOW = max(POOL_WINDOWS)
DIFF_HEADS = 4
DIFF_HD = 128
DIFF_VD = 2 * DIFF_HD
MEM_HEADS = 4
ROT_DIM = DIFF_HD // 4
ROPE_THETA = 500000.0
NORM_EPS = 1e-6
SUBLN_EPS = 1e-5
LOG2E = math.log2(math.e)
LANES = 128

VMEM_LIMIT_BYTES = 56 * 1024 * 1024

TOKEN_TILE = 512
FRONT_TILE = 1024
ATTN_TILE = 512
ATTN_STRIP = 32
FF_CHUNK = 256
COL_BLOCK = 512


def _params(n_axes):
    return pltpu.CompilerParams(
        dimension_semantics=("arbitrary",) * n_axes,
        vmem_limit_bytes=VMEM_LIMIT_BYTES,
    )


def _resident(shape):
    zeros = (0,) * len(shape)
    return pl.BlockSpec(shape, lambda *_: zeros, pipeline_mode=pl.Buffered(1))


def _rmsnorm(x, g, eps):
    ms = jnp.mean(x * x, axis=-1, keepdims=True)
    return x * lax.rsqrt(ms + eps) * g


def _dot(a, b):
    return jnp.dot(a, b, preferred_element_type=jnp.float32)


def _dot_nt(a, b):
    return lax.dot_general(a, b, (((1,), (1,)), ((), ())),
                           preferred_element_type=jnp.float32)


def _rope(t, cos_t, sin_lo, sin_hi):
    outs = []
    for j in range(t.shape[1] // LANES):
        tj = t[:, j * LANES:(j + 1) * LANES]
        up = pltpu.roll(tj, LANES - ROT_DIM // 2, axis=1)
        dn = pltpu.roll(tj, ROT_DIM // 2, axis=1)
        outs.append(tj * cos_t + up * sin_lo + dn * sin_hi)
    return jnp.concatenate(outs, axis=1)


def _rope_tables(seq_len):
    pos = jnp.arange(seq_len, dtype=jnp.float32)
    inv_freq = jnp.power(jnp.float32(ROPE_THETA),
                         -jnp.arange(0, ROT_DIM, 2, dtype=jnp.float32) / ROT_DIM)
    ang = pos[:, None] * inv_freq[None, :]
    cos, sin = jnp.cos(ang), jnp.sin(ang)
    half = ROT_DIM // 2
    ones = jnp.ones((seq_len, LANES - ROT_DIM), jnp.float32)
    cos_t = jnp.concatenate([cos, cos, ones], axis=1)
    sin_lo = jnp.concatenate([-sin, jnp.zeros((seq_len, LANES - half), jnp.float32)], axis=1)
    sin_hi = jnp.concatenate([jnp.zeros((seq_len, half), jnp.float32), sin,
                              jnp.zeros((seq_len, LANES - ROT_DIM), jnp.float32)], axis=1)
    return cos_t, sin_lo, sin_hi


def _mem_kv_kernel(mem_ref, g_ref, w_ref, o_ref):
    mem_w = o_ref.shape[3] // 2
    xn = _rmsnorm(mem_ref[0], g_ref[0], NORM_EPS).astype(jnp.bfloat16)
    kv = _dot(xn, w_ref[0])
    k_scale = (mem_w // MEM_HEADS) ** -0.5 * LOG2E
    o_ref[0, 0, :, :mem_w] = (kv[:, :mem_w] * k_scale).astype(jnp.bfloat16)
    o_ref[0, 0, :, mem_w:] = kv[:, mem_w:].astype(jnp.bfloat16)


def _mem_kv(mem, g_mem, w_mem_kv):
    b, m, d = mem.shape
    depth, _, n = w_mem_kv.shape
    return pl.pallas_call(
        _mem_kv_kernel,
        grid=(depth, b),
        in_specs=[
            pl.BlockSpec((1, m, d), lambda l, i: (i, 0, 0)),
            pl.BlockSpec((1, 1, d), lambda l, i: (l, 0, 0)),
            pl.BlockSpec((1, d, n), lambda l, i: (l, 0, 0)),
        ],
        out_specs=pl.BlockSpec((1, 1, m, n), lambda l, i: (l, i, 0, 0)),
        out_shape=jax.ShapeDtypeStruct((depth, b, m, n), jnp.bfloat16),
        compiler_params=_params(2),
        name="mem_kv",
    )(mem, g_mem.reshape(depth, 1, d), w_mem_kv)


def _project(xn_sc, w_ref, col0, z_sc):
    for j in range(z_sc.shape[1] // COL_BLOCK):
        cols = slice(j * COL_BLOCK, (j + 1) * COL_BLOCK)
        z_sc[:, cols] = _dot(xn_sc[...], w_ref[:, col0 + j * COL_BLOCK:col0 + (j + 1) * COL_BLOCK])


def _memory_attention(zq_sc, kv_ref, mo_ref):
    mem_w = mo_ref.shape[2]
    hd = mem_w // MEM_HEADS
    for hh in range(MEM_HEADS):
        cols = slice(hh * hd, (hh + 1) * hd)
        km = kv_ref[0, 0, :, cols]
        vm = kv_ref[0, 0, :, mem_w + hh * hd:mem_w + (hh + 1) * hd]
        s = _dot_nt(zq_sc[:, cols].astype(jnp.bfloat16), km)
        e = jnp.exp2(s - jnp.max(s, axis=-1, keepdims=True))
        p = e / jnp.sum(e, axis=-1, keepdims=True)
        mo_ref[0, :, cols] = _dot(p.astype(jnp.bfloat16), vm).astype(jnp.bfloat16)


def _in_pool_kernel(h_ref, g_ref, w_in_ref, pw_ref, ps_ref, kv_ref,
                    mix_ref, mo_ref, xn_sc, zbuf, zq_sc, *, tm):
    si = pl.program_id(1)
    mix_w = mix_ref.shape[2]
    pool_c = mix_w // len(POOL_WINDOWS)
    xn_sc[...] = _rmsnorm(h_ref[0], g_ref[...], NORM_EPS).astype(jnp.bfloat16)

    @pl.when(si == 0)
    def _():
        zbuf[0:MAX_WINDOW, :] = jnp.zeros((MAX_WINDOW, mix_w), jnp.float32)

    _project(xn_sc, w_in_ref, 0, zbuf.at[MAX_WINDOW:MAX_WINDOW + tm, :])
    _project(xn_sc, w_in_ref, mix_w, zq_sc)
    t = si * tm + lax.broadcasted_iota(jnp.int32, (tm, 1), 0)
    for g, w in enumerate(POOL_WINDOWS):
        cols = slice(g * pool_c, (g + 1) * pool_c)
        acc = zbuf[:, cols]
        shift = 1
        while shift < w:
            acc = acc + pltpu.roll(acc, shift, axis=0)
            shift *= 2
        cnt = jnp.minimum(t + 1, w).astype(jnp.float32)
        pooled = acc[MAX_WINDOW:, :] / cnt - zbuf[MAX_WINDOW:MAX_WINDOW + tm, cols]
        zbuf[0:MAX_WINDOW, cols] = zbuf[tm:tm + MAX_WINDOW, cols]
        mix_ref[0, :, cols] = (_dot(pooled.astype(jnp.bfloat16), pw_ref[g])
                               * ps_ref[:, cols]).astype(jnp.bfloat16)
    _memory_attention(zq_sc, kv_ref, mo_ref)


def _in_pool(h, g, w_in, pool_w, pool_scale, mem_kv, layer, tm):
    b, s, d = h.shape
    in_w = w_in.shape[1]
    mix_w = pool_scale.shape[0]
    mem_w = in_w - mix_w
    m, kvn = mem_kv.shape[2], mem_kv.shape[3]
    tok = lambda i, j: (i, j, 0)
    return pl.pallas_call(
        functools.partial(_in_pool_kernel, tm=tm),
        grid=(b, s // tm),
        in_specs=[
            pl.BlockSpec((1, tm, d), tok),
            _resident((1, d)),
            _resident((d, in_w)),
            _resident(pool_w.shape),
            _resident((1, mix_w)),
            pl.BlockSpec((1, 1, m, kvn), lambda i, j: (layer, i, 0, 0)),
        ],
        out_specs=[pl.BlockSpec((1, tm, mix_w), tok), pl.BlockSpec((1, tm, mem_w), tok)],
        out_shape=[jax.ShapeDtypeStruct((b, s, mix_w), jnp.bfloat16),
                   jax.ShapeDtypeStruct((b, s, mem_w), jnp.bfloat16)],
        scratch_shapes=[pltpu.VMEM((tm, d), jnp.bfloat16),
                        pltpu.VMEM((tm + MAX_WINDOW, mix_w), jnp.float32),
                        pltpu.VMEM((tm, mem_w), jnp.float32)],
        compiler_params=_params(2),
        name="in_pool",
    )(h, g.reshape(1, d), w_in, pool_w, pool_scale.reshape(1, mix_w), mem_kv)


def _in_diff_kernel(h_ref, g_ref, w_in_ref, cos_ref, slo_ref, shi_ref, kv_ref,
                    q_ref, mo_ref, xn_sc, zmix_sc, zq_sc):
    mix_w = q_ref.shape[2]
    xn_sc[...] = _rmsnorm(h_ref[0], g_ref[...], NORM_EPS).astype(jnp.bfloat16)
    _project(xn_sc, w_in_ref, 0, zmix_sc)
    _project(xn_sc, w_in_ref, mix_w, zq_sc)
    for j in range(mix_w // COL_BLOCK):
        cols = slice(j * COL_BLOCK, (j + 1) * COL_BLOCK)
        q = _rope(zmix_sc[:, cols], cos_ref[...], slo_ref[...], shi_ref[...])
        q_ref[0, :, cols] = (q * (DIFF_HD ** -0.5 * LOG2E)).astype(jnp.bfloat16)
    _memory_attention(zq_sc, kv_ref, mo_ref)


def _in_diff(h, g, w_in, mix_w, rope, mem_kv, layer, tm):
    b, s, d = h.shape
    in_w = w_in.shape[1]
    mem_w = in_w - mix_w
    m, kvn = mem_kv.shape[2], mem_kv.shape[3]
    tok = lambda i, j: (i, j, 0)
    tab = pl.BlockSpec((tm, LANES), lambda i, j: (j, 0))
    return pl.pallas_call(
        _in_diff_kernel,
        grid=(b, s // tm),
        in_specs=[
            pl.BlockSpec((1, tm, d), tok),
            _resident((1, d)),
            _resident((d, in_w)),
            tab, tab, tab,
            pl.BlockSpec((1, 1, m, kvn), lambda i, j: (layer, i, 0, 0)),
        ],
        out_specs=[pl.BlockSpec((1, tm, mix_w), tok), pl.BlockSpec((1, tm, mem_w), tok)],
        out_shape=[jax.ShapeDtypeStruct((b, s, mix_w), jnp.bfloat16),
                   jax.ShapeDtypeStruct((b, s, mem_w), jnp.bfloat16)],
        scratch_shapes=[pltpu.VMEM((tm, d), jnp.bfloat16),
                        pltpu.VMEM((tm, mix_w), jnp.float32),
                        pltpu.VMEM((tm, mem_w), jnp.float32)],
        compiler_params=_params(2),
        name="in_diff",
    )(h, g.reshape(1, d), w_in, *rope, mem_kv)


def _shared_kv_kernel(h_ref, g_ref, w_ref, cos_ref, slo_ref, shi_ref, k_ref, v_ref,
                      xn_sc, zk_sc):
    kw, vw = k_ref.shape[2], v_ref.shape[2]
    xn_sc[...] = _rmsnorm(h_ref[0], g_ref[...], NORM_EPS).astype(jnp.bfloat16)
    _project(xn_sc, w_ref, 0, zk_sc)
    for j in range(vw // COL_BLOCK):
        cols = slice(kw + j * COL_BLOCK, kw + (j + 1) * COL_BLOCK)
        v_ref[0, :, j * COL_BLOCK:(j + 1) * COL_BLOCK] = _dot(
            xn_sc[...], w_ref[:, cols]).astype(jnp.bfloat16)
    for j in range(kw // COL_BLOCK):
        cols = slice(j * COL_BLOCK, (j + 1) * COL_BLOCK)
        k = _rope(zk_sc[:, cols], cos_ref[...], slo_ref[...], shi_ref[...])
        k_ref[0, :, cols] = k.astype(jnp.bfloat16)


def _shared_kv(h, g, w_kv, rope, tm):
    b, s, d = h.shape
    n = w_kv.shape[1]
    kw = DIFF_HEADS * 2 * DIFF_HD
    vw = n - kw
    tok = lambda i, j: (i, j, 0)
    tab = pl.BlockSpec((tm, LANES), lambda i, j: (j, 0))
    return pl.pallas_call(
        _shared_kv_kernel,
        grid=(b, s // tm),
        in_specs=[pl.BlockSpec((1, tm, d), tok), _resident((1, d)), _resident((d, n)),
                  tab, tab, tab],
        out_specs=[pl.BlockSpec((1, tm, kw), tok), pl.BlockSpec((1, tm, vw), tok)],
        out_shape=[jax.ShapeDtypeStruct((b, s, kw), jnp.bfloat16),
                   jax.ShapeDtypeStruct((b, s, vw), jnp.bfloat16)],
        scratch_shapes=[pltpu.VMEM((tm, d), jnp.bfloat16), pltpu.VMEM((tm, kw), jnp.float32)],
        compiler_params=_params(2),
        name="shared_kv",
    )(h, g.reshape(1, d), w_kv, *rope)


def _attn_items(nq):
    rows = []
    for t in range(nq):
        keys = [t] + list(range(t))
        for n, j in enumerate(keys):
            rows.append((t, j, int(j == t), int(n == 0), int(n == len(keys) - 1)))
    return np.asarray(rows, np.int32).T


def _diff_attn_kernel(tbl, q_ref, k_ref, v_ref, bias_ref, lq1_ref, lk1_ref, lq2_ref, lk2_ref,
                      gs_ref, o_ref, s0, s1, p0, p1, a0, a1, m_sc, l_sc, acc_sc,
                      *, tq, n_items, lambda_init):
    m_sc[...] = jnp.zeros(m_sc.shape, jnp.float32)
    l_sc[...] = jnp.zeros(l_sc.shape, jnp.float32)
    acc_sc[...] = jnp.zeros(acc_sc.shape, jnp.float32)
    n_strips = tq // ATTN_STRIP
    n_chunks = tq // LANES
    s_buf, p_buf, a_buf = (s0, s1), (p0, p1), (a0, a1)

    def tile_rows(idx):
        return pl.ds(pl.multiple_of(idx * tq, tq), tq)

    def scores(t, slot):
        qrows, krows = tile_rows(tbl[0, t]), tile_rows(tbl[1, t])
        for c in range(2):
            cols = slice(c * DIFF_HD, (c + 1) * DIFF_HD)
            s_buf[slot][c] = _dot_nt(q_ref[0, qrows, cols], k_ref[0, krows, cols])

    def softmax(t, slot):
        diag = tbl[2, t]
        first = tbl[3, t] == 1
        par = tbl[0, t] % 2
        for c in range(2):
            for r in range(n_strips):
                rows = slice(r * ATTN_STRIP, (r + 1) * ATTN_STRIP)
                sv = s_buf[slot][c, rows, :] + bias_ref[diag, rows, :]
                m_old = jnp.where(first, -jnp.inf, m_sc[par, c, rows, :])
                m_new = jnp.maximum(m_old, jnp.max(sv, axis=-1, keepdims=True))
                alpha = jnp.exp2(m_old - m_new)
                p = jnp.exp2(sv - jnp.concatenate([m_new] * n_chunks, axis=1))
                psum = p[:, 0:LANES]
                for k in range(1, n_chunks):
                    psum = psum + p[:, k * LANES:(k + 1) * LANES]
                l_sc[par, c, rows, :] = alpha * l_sc[par, c, rows, :] + psum
                m_sc[par, c, rows, :] = m_new
                a_buf[slot][c, rows, :] = alpha
                p_buf[slot][c, rows, :] = p.astype(jnp.bfloat16)

    def values(t, slot):
        v = v_ref[0, tile_rows(tbl[1, t]), :]
        par = tbl[0, t] % 2
        for c in range(2):
            alpha = a_buf[slot][c]
            acc_sc[par, c] = (jnp.concatenate([alpha] * (DIFF_VD // LANES), axis=1)
                              * acc_sc[par, c] + _dot(p_buf[slot][c], v))

    def finalize(t):
        @pl.when(tbl[4, t] == 1)
        def _():
            lam = (jnp.exp(jnp.sum(lq1_ref[...] * lk1_ref[...], keepdims=True))
                   - jnp.exp(jnp.sum(lq2_ref[...] * lk2_ref[...], keepdims=True))
                   + lambda_init)
            par = tbl[0, t] % 2
            o1 = acc_sc[par, 0] / jnp.sum(l_sc[par, 0], axis=-1, keepdims=True)
            o2 = acc_sc[par, 1] / jnp.sum(l_sc[par, 1], axis=-1, keepdims=True)
            o = _rmsnorm(o1 - lam * o2, gs_ref[...], SUBLN_EPS) * (1.0 - lambda_init)
            o_ref[0, tile_rows(tbl[0, t]), :] = o.astype(jnp.bfloat16)

    def tick(i, parity):
        if not isinstance(i, int) or 0 <= i - 2 < n_items:
            values(i - 2, parity)
        if not isinstance(i, int) or 0 <= i - 1 < n_items:
            softmax(i - 1, 1 - parity)
        if not isinstance(i, int) or 0 <= i < n_items:
            scores(i, parity)

    def ticks(first, count):
        for d in range(count):
            tick(first + d, d % 2 if not isinstance(first, int) else (first + d) % 2)
        for d in range(count):
            if not isinstance(first, int) or 0 <= first + d - 2 < n_items:
                finalize(first + d - 2)

    ticks(0, 2)
    n_pairs = (n_items - 2) // 2
    lax.fori_loop(0, n_pairs, lambda k, carry: (ticks(2 + 2 * k, 2), carry)[1], 0)
    for i in range(2 + 2 * n_pairs, n_items + 2):
        ticks(i, 1)


def _diff_attention(q, k, v, lq1, lk1, lq2, lk2, g_sub, lambda_init, tq):
    b, s, w = q.shape
    hw = 2 * DIFF_HD
    items = _attn_items(s // tq)
    n_items = items.shape[1]
    assert n_items >= 2
    r = np.arange(tq)
    bias = np.zeros((2, tq, tq), np.float32)
    bias[1] = np.where(r[None, :] <= r[:, None], 0.0, -np.inf)
    vec = lambda a: a.reshape(1, -1)
    head = lambda i, h, tbl: (i, 0, h)
    const2 = lambda i, h, tbl: (0, 0)
    grid_spec = pltpu.PrefetchScalarGridSpec(
        num_scalar_prefetch=1,
        grid=(b, DIFF_HEADS),
        in_specs=[
            pl.BlockSpec((1, s, hw), head),
            pl.BlockSpec((1, s, hw), head),
            pl.BlockSpec((1, s, DIFF_VD), head),
            pl.BlockSpec((2, tq, tq), lambda i, h, tbl: (0, 0, 0),
                         pipeline_mode=pl.Buffered(1)),
            pl.BlockSpec((1, DIFF_HD), const2), pl.BlockSpec((1, DIFF_HD), const2),
            pl.BlockSpec((1, DIFF_HD), const2), pl.BlockSpec((1, DIFF_HD), const2),
            pl.BlockSpec((1, DIFF_VD), const2),
        ],
        out_specs=pl.BlockSpec((1, s, DIFF_VD), head),
        scratch_shapes=[
            pltpu.VMEM((2, tq, tq), jnp.float32), pltpu.VMEM((2, tq, tq), jnp.float32),
            pltpu.VMEM((2, tq, tq), jnp.bfloat16), pltpu.VMEM((2, tq, tq), jnp.bfloat16),
            pltpu.VMEM((2, tq, LANES), jnp.float32), pltpu.VMEM((2, tq, LANES), jnp.float32),
            pltpu.VMEM((2, 2, tq, LANES), jnp.float32),
            pltpu.VMEM((2, 2, tq, LANES), jnp.float32),
            pltpu.VMEM((2, 2, tq, DIFF_VD), jnp.float32),
        ],
    )
    return pl.pallas_call(
        functools.partial(_diff_attn_kernel, tq=tq, n_items=n_items, lambda_init=lambda_init),
        grid_spec=grid_spec,
        out_shape=jax.ShapeDtypeStruct((b, s, w), jnp.bfloat16),
        compiler_params=_params(2),
        name="diff_attn",
    )(jnp.asarray(items), q, k, v, jnp.asarray(bias), vec(lq1), vec(lk1), vec(lq2), vec(lk2),
      vec(g_sub))


def _out_ffn_kernel(h_ref, mix_ref, mo_ref, w_out_ref, g_ref, wg_ref, wu_ref, wd_ref,
                    gf_ref, o_ref, h1_sc, xn_sc, act_sc, *, final_norm):
    mix_w = mix_ref.shape[2]
    d_ff = wg_ref.shape[1]
    h1_sc[...] = (h_ref[0] + _dot(mix_ref[0], w_out_ref[0:mix_w, :])
                  + _dot(mo_ref[0], w_out_ref[mix_w:, :]))
    xn_sc[...] = _rmsnorm(h1_sc[...], g_ref[...], NORM_EPS).astype(jnp.bfloat16)
    for c in range(d_ff // FF_CHUNK):
        cols = slice(c * FF_CHUNK, (c + 1) * FF_CHUNK)
        gate = _dot(xn_sc[...], wg_ref[:, cols])
        up = _dot(xn_sc[...], wu_ref[:, cols])
        act_sc[:, cols] = (gate * jax.nn.sigmoid(gate) * up).astype(jnp.bfloat16)
    h2 = h1_sc[...] + _dot(act_sc[...], wd_ref[...])
    if final_norm:
        h2 = _rmsnorm(h2, gf_ref[...], NORM_EPS)
    o_ref[0] = h2


def _out_ffn(h, mix, mo, w_out, g_ffn, w_gate, w_up, w_down, g_final, final_norm, tm):
    b, s, d = h.shape
    d_ff = w_gate.shape[1]
    tok = lambda i, j: (i, j, 0)
    return pl.pallas_call(
        functools.partial(_out_ffn_kernel, final_norm=final_norm),
        grid=(b, s // tm),
        scratch_shapes=[pltpu.VMEM((tm, d), jnp.float32), pltpu.VMEM((tm, d), jnp.bfloat16),
                        pltpu.VMEM((tm, d_ff), jnp.bfloat16)],
        in_specs=[
            pl.BlockSpec((1, tm, d), tok),
            pl.BlockSpec((1, tm, mix.shape[2]), tok),
            pl.BlockSpec((1, tm, mo.shape[2]), tok),
            _resident(w_out.shape),
            _resident((1, d)),
            _resident(w_gate.shape), _resident(w_up.shape), _resident(w_down.shape),
            _resident((1, d)),
        ],
        out_specs=pl.BlockSpec((1, tm, d), tok),
        out_shape=jax.ShapeDtypeStruct((b, s, d), jnp.float32),
        compiler_params=_params(2),
        name="out_ffn",
    )(h, mix, mo, w_out, g_ffn.reshape(1, d), w_gate, w_up, w_down, g_final.reshape(1, d))


def kernel(x, mem, w_in, w_out, g_mix, g_ffn, w_gate, w_up, w_down, g_mem, w_mem_kv,
           pool_w, pool_scale, g_kv, w_kv, lam_q1, lam_k1, lam_q2, lam_k2, g_subln, g_final):
    b, s, d = x.shape
    depth = w_in.shape[0]
    n_a = pool_w.shape[0]
    mix_w = pool_scale.shape[1]
    d_ff = w_gate.shape[2]
    tm = min(TOKEN_TILE, s)
    tf = min(FRONT_TILE, s)
    tq = min(ATTN_TILE, s)
    assert s % tm == 0 and s % tf == 0 and s % tq == 0 and d_ff % FF_CHUNK == 0

    bf = lambda a: a.astype(jnp.bfloat16)
    wg, wu, wd = bf(w_gate), bf(w_up), bf(w_down)
    w_in_b, w_out_b, pool_w_b, w_kv_b = bf(w_in), bf(w_out), bf(pool_w), bf(w_kv)

    rope = _rope_tables(s)
    mem_kv = _mem_kv(mem, g_mem, bf(w_mem_kv))

    h = x
    k_sh = v_sh = None
    for i in range(depth):
        if i < n_a:
            mix, mo = _in_pool(h, g_mix[i], w_in_b[i], pool_w_b[i], pool_scale[i], mem_kv, i, tf)
        else:
            j = i - n_a
            lambda_init = 0.8 - 0.6 * math.exp(-0.3 * i)
            q, mo = _in_diff(h, g_mix[i], w_in_b[i], mix_w, rope, mem_kv, i, tf)
            mix = _diff_attention(q, k_sh, v_sh, lam_q1[j], lam_k1[j], lam_q2[j], lam_k2[j],
                                  g_subln[j], lambda_init, tq)
        h = _out_ffn(h, mix, mo, w_out_b[i], g_ffn[i], wg[i], wu[i], wd[i], g_final,
                     i == depth - 1, tm)
        if i == n_a - 1:
            k_sh, v_sh = _shared_kv(h, g_kv, w_kv_b, rope, tf)
    return h
```
